```python
import math
import jax, jax.numpy as jnp
from jax import lax
import numpy as np

D_MODEL = 4096
BATCH = 4
SEQ = 2048
DEPTH = 2
DEC_BATCH = 8
DEC_SEQ = 4
PAST_LEN = 16384
PAGE_SIZE = 128

HEAD_DIM = 128
POOL_WIDTH = D_MODEL // 4
POOL_WINDOWS = (2, 4, 8, 16)
POOL_GROUPS = len(POOL_WINDOWS)
POOL_GC = POOL_WIDTH // POOL_GROUPS
POOL_STATE = max(POOL_WINDOWS) - 1
DIFF_WIDTH = (D_MODEL - POOL_WIDTH) // 2
DIFF_HEADS = DIFF_WIDTH // HEAD_DIM
DIFF_QK = HEAD_DIM // 2
SB_WIDTH = D_MODEL - POOL_WIDTH - DIFF_WIDTH
SB_HEADS = SB_WIDTH // HEAD_DIM
IN_WIDTH = POOL_WIDTH + 3 * DIFF_WIDTH + 3 * SB_WIDTH
MEM_TOKENS = 256
XA_HEADS = 4
XA_HEAD_DIM = 128
XA_WIDTH = XA_HEADS * XA_HEAD_DIM
D_FF = 4 * D_MODEL
Q_BLOCK = 128
ROPE_THETA = 10000.0
EPS = 1e-6
NEG_INF = -1e30

kernel_name = 'hybrid_pool_diff_stickbreak_decoder_step'


def _rms(x, g):
    xf = x.astype(jnp.float32)
    y = xf * lax.rsqrt(jnp.mean(xf * xf, axis=-1, keepdims=True) + EPS)
    return (y * g.astype(jnp.float32)).astype(x.dtype)


def _rope(x, pos):
    half = x.shape[-1] // 2
    inv = ROPE_THETA ** (-jnp.arange(half, dtype=jnp.float32) / half)
    ang = pos.astype(jnp.float32)[:, None] * inv[None, :]
    bshape = (1, pos.shape[0]) + (1,) * (x.ndim - 3) + (half,)
    cos = jnp.cos(ang).reshape(bshape)
    sin = jnp.sin(ang).reshape(bshape)
    xf = x.astype(jnp.float32)
    x1, x2 = xf[..., :half], xf[..., half:]
    return jnp.concatenate([x1 * cos - x2 * sin, x2 * cos + x1 * sin], axis=-1).astype(x.dtype)


def _project(xn, w_in):
    B, T, _ = xn.shape
    y = xn @ w_in
    bounds = np.cumsum([POOL_WIDTH] + [DIFF_WIDTH] * 3 + [SB_WIDTH] * 3)[:-1].tolist()
    u, qd, kd, vd, qs, ks, vs = jnp.split(y, bounds, axis=-1)
    qd = qd.reshape(B, T, DIFF_HEADS, 2, DIFF_QK)
    kd = kd.reshape(B, T, DIFF_HEADS, 2, DIFF_QK)
    vd = vd.reshape(B, T, DIFF_HEADS, HEAD_DIM)
    qs = qs.reshape(B, T, SB_HEADS, HEAD_DIM)
    ks = ks.reshape(B, T, SB_HEADS, HEAD_DIM)
    vs = vs.reshape(B, T, SB_HEADS, HEAD_DIM)
    return u, qd, kd, vd, qs, ks, vs


def _pool_mix(u, prefix, pos, w_pool, scale):
    B, T, _ = u.shape
    up = jnp.concatenate([prefix, u], axis=1).astype(jnp.float32)
    csum = jnp.concatenate([jnp.zeros((B, 1, POOL_WIDTH), jnp.float32), jnp.cumsum(up, axis=1)], axis=1)
    hi = csum[:, POOL_STATE + 1:POOL_STATE + 1 + T]
    groups = []
    for g, w in enumerate(POOL_WINDOWS):
        sl = slice(g * POOL_GC, (g + 1) * POOL_GC)
        lo = csum[:, POOL_STATE + 1 - w:POOL_STATE + 1 - w + T, sl]
        cnt = jnp.minimum(pos + 1, w).astype(jnp.float32)[None, :, None]
        groups.append((hi[..., sl] - lo) / cnt - up[:, POOL_STATE:, sl])
    r = jnp.stack(groups, axis=2).astype(u.dtype)
    y = jnp.einsum('btgc,gce->btge', r, w_pool).reshape(B, T, POOL_WIDTH)
    return y * scale


def _apply_weights(w, vs):
    out = None
    start = 0
    for v in vs:
        n = v.shape[1]
        term = jnp.einsum('bhqk,bkhe->bqhe', w[..., start:start + n].astype(v.dtype), v)
        out = term if out is None else out + term
        start += n
    return out


def _diff_mix(q, ks, vs, masks, lam):
    s = jnp.concatenate([jnp.einsum('bqhcd,bkhcd->bchqk', q, k).astype(jnp.float32) for k in ks], axis=-1)
    s = s * (DIFF_QK ** -0.5)
    m = jnp.concatenate(masks, axis=-1)
    p = jax.nn.softmax(jnp.where(m, s, NEG_INF), axis=-1)
    w = p[:, 0] - lam * p[:, 1]
    return _apply_weights(w, vs)


def _stick_mix(q, ks, vs, masks):
    z = jnp.concatenate([jnp.einsum('bqhd,bkhd->bhqk', q, k).astype(jnp.float32) for k in ks], axis=-1)
    z = z * (HEAD_DIM ** -0.5)
    m = jnp.concatenate(masks, axis=-1)
    log_beta = jax.nn.log_sigmoid(z)
    log_keep = jnp.where(m, jax.nn.log_sigmoid(-z), 0.0)
    after = lax.cumsum(log_keep, axis=3, reverse=True) - log_keep
    a = jnp.where(m, jnp.exp(log_beta + after), 0.0)
    return _apply_weights(a, vs)


def _query_blocks(fn, q, qpos):
    B, T = q.shape[0], q.shape[1]
    nb = T // Q_BLOCK
    qb = jnp.moveaxis(q.reshape((B, nb, Q_BLOCK) + q.shape[2:]), 1, 0)
    pb = qpos.reshape(nb, Q_BLOCK)
    out = lax.map(lambda a: fn(a[0], a[1]), (qb, pb))
    return jnp.moveaxis(out, 0, 1).reshape((B, T) + out.shape[3:])


def _merge(a_out, b_out, c_out, w_out):
    B, T = a_out.shape[0], a_out.shape[1]
    cat = jnp.concatenate([a_out, b_out.reshape(B, T, DIFF_WIDTH), c_out.reshape(B, T, SB_WIDTH)], axis=-1)
    return cat @ w_out


def _mem_kv(mem, g_mem, w_kv):
    B, M, _ = mem.shape
    kv = _rms(mem, g_mem) @ w_kv
    k = kv[..., :XA_WIDTH].reshape(B, M, XA_HEADS, XA_HEAD_DIM)
    v = kv[..., XA_WIDTH:].reshape(B, M, XA_HEADS, XA_HEAD_DIM)
    return k, v


def _cross(xn, mk, mv, w_q, w_o):
    B, T, _ = xn.shape
    q = (xn @ w_q).reshape(B, T, XA_HEADS, XA_HEAD_DIM)
    s = jnp.einsum('bthd,bmhd->bhtm', q, mk).astype(jnp.float32) * (XA_HEAD_DIM ** -0.5)
    p = jax.nn.softmax(s, axis=-1)
    o = jnp.einsum('bhtm,bmhd->bthd', p.astype(mv.dtype), mv).reshape(B, T, XA_WIDTH)
    return o @ w_o


def _mlp(xn, w_up, w_down):
    h = jnp.square(jax.nn.relu(xn @ w_up))
    return h @ w_down


def setup_inputs(seed: int = 0) -> dict:
    key = jax.random.key(seed)
    k = jax.random.split(key, 32)
    f32 = jnp.float32
    n_pages = PAST_LEN // PAGE_SIZE
    n_used = DEC_BATCH * n_pages
    n_pool = n_used + (n_used + 3) // 4

    def nrm(kk, shape, s=1.0):
        return jax.random.normal(kk, shape, f32) * s

    cache_shape = (DEPTH, n_pool, PAGE_SIZE, DIFF_HEADS, HEAD_DIM)
    sb_shape = (DEPTH, n_pool, PAGE_SIZE, SB_HEADS, HEAD_DIM)
    page_table = jax.random.permutation(k[10], n_pool)[:n_used].reshape(DEC_BATCH, n_pages).astype(jnp.int32)
    return {
        'x_prompt': nrm(k[0], (BATCH, SEQ, D_MODEL)),
        'x_sample': nrm(k[1], (DEC_BATCH, DEC_SEQ, D_MODEL)),
        'mem_prompt': nrm(k[2], (BATCH, MEM_TOKENS, D_MODEL)),
        'cache_k_diff': nrm(k[3], cache_shape),
        'cache_v_diff': nrm(k[4], cache_shape),
        'cache_k_sb': nrm(k[5], sb_shape),
        'cache_v_sb': nrm(k[6], sb_shape),
        'state_pool': nrm(k[7], (DEPTH, DEC_BATCH, POOL_STATE, POOL_WIDTH)),
        'cache_mem_k': nrm(k[8], (DEPTH, DEC_BATCH, MEM_TOKENS, XA_HEADS, XA_HEAD_DIM)),
        'cache_mem_v': nrm(k[9], (DEPTH, DEC_BATCH, MEM_TOKENS, XA_HEADS, XA_HEAD_DIM)),
        'page_table': page_table,
        'g_mix': 1.0 + nrm(k[11], (DEPTH, D_MODEL), 0.05),
        'w_in': nrm(k[12], (DEPTH, D_MODEL, IN_WIDTH), D_MODEL ** -0.5),
        'w_pool': nrm(k[13], (DEPTH, POOL_GROUPS, POOL_GC, POOL_GC), POOL_GC ** -0.5),
        'pool_scale': 1.0 + nrm(k[14], (DEPTH, POOL_WIDTH), 0.1),
        'lam': nrm(k[15], (DEPTH, 4, DIFF_QK), 0.1),
        'g_subln': 1.0 + nrm(k[16], (DEPTH, HEAD_DIM), 0.05),
        'w_out': nrm(k[17], (DEPTH, D_MODEL, D_MODEL), D_MODEL ** -0.5),
        'g_x': 1.0 + nrm(k[18], (DEPTH, D_MODEL), 0.05),
        'g_mem': 1.0 + nrm(k[19], (DEPTH, D_MODEL), 0.05),
        'w_q_x': nrm(k[20], (DEPTH, D_MODEL, XA_WIDTH), D_MODEL ** -0.5),
        'w_kv_x': nrm(k[21], (DEPTH, D_MODEL, 2 * XA_WIDTH), D_MODEL ** -0.5),
        'w_o_x': nrm(k[22], (DEPTH, XA_WIDTH, D_MODEL), XA_WIDTH ** -0.5),
        'g_mlp': 1.0 + nrm(k[23], (DEPTH, D_MODEL), 0.05),
        'w_up': nrm(k[24], (DEPTH, D_MODEL, D_FF), D_MODEL ** -0.5),
        'w_down': nrm(k[25], (DEPTH, D_FF, D_MODEL), (1.5 * D_FF) ** -0.5),
        'g_final': 1.0 + nrm(k[26], (D_MODEL,), 0.05),
    }


def reference(x_prompt, x_sample, mem_prompt, cache_k_diff, cache_v_diff, cache_k_sb, cache_v_sb,
              state_pool, cache_mem_k, cache_mem_v, page_table, g_mix, w_in, w_pool, pool_scale, lam,
              g_subln, w_out, g_x, g_mem, w_q_x, w_kv_x, w_o_x, g_mlp, w_up, w_down, g_final):
    B, T, _ = x_prompt.shape
    DB, TQ, _ = x_sample.shape
    past = page_table.shape[1] * PAGE_SIZE
    pos_p = jnp.arange(T, dtype=jnp.int32)
    pos_s = past + jnp.arange(TQ, dtype=jnp.int32)
    causal_new = pos_s[None, :] <= pos_s[:, None]
    strict_new = pos_s[None, :] < pos_s[:, None]
    past_vis = jnp.ones((TQ, past), dtype=bool)
    prefix_zero = jnp.zeros((B, POOL_STATE, POOL_WIDTH), x_prompt.dtype)

    xp, xs = x_prompt, x_sample
    pk_d, pv_d, pk_s, pv_s, p_pool, p_mk, p_mv = [], [], [], [], [], [], []
    sk_d, sv_d, sk_s, sv_s, s_pool = [], [], [], [], []

    for l in range(DEPTH):
        lam_init = 0.8 - 0.6 * math.exp(-0.3 * l)
        lam_f = lam[l].astype(jnp.float32)
        lam_l = jnp.exp(jnp.sum(lam_f[0] * lam_f[1])) - jnp.exp(jnp.sum(lam_f[2] * lam_f[3])) + lam_init

        u, qd, kd, vd, qs, ks_, vs_ = _project(_rms(xp, g_mix[l]), w_in[l])
        qd = _rope(qd, pos_p)
        kd = _rope(kd, pos_p)
        a_out = _pool_mix(u, prefix_zero.astype(u.dtype), pos_p, w_pool[l], pool_scale[l])
        b_out = _query_blocks(
            lambda qb, pb: _diff_mix(qb, (kd,), (vd,), (pos_p[None, :] <= pb[:, None],), lam_l), qd, pos_p)
        b_out = _rms(b_out, g_subln[l]) * (1.0 - lam_init)
        c_out = _query_blocks(
            lambda qb, pb: _stick_mix(qb, (ks_,), (vs_,), (pos_p[None, :] < pb[:, None],)), qs, pos_p)
        xp = xp + _merge(a_out, b_out, c_out, w_out[l])
        pk_d.append(kd.reshape(B, T, DIFF_HEADS, HEAD_DIM))
        pv_d.append(vd)
        pk_s.append(ks_)
        pv_s.append(vs_)
        p_pool.append(u[:, T - POOL_STATE:])
        mk, mv = _mem_kv(mem_prompt, g_mem[l], w_kv_x[l])
        p_mk.append(mk)
        p_mv.append(mv)
        xp = xp + _cross(_rms(xp, g_x[l]), mk, mv, w_q_x[l], w_o_x[l])
        xp = xp + _mlp(_rms(xp, g_mlp[l]), w_up[l], w_down[l])

        u, qd, kd, vd, qs, ks_, vs_ = _project(_rms(xs, g_mix[l]), w_in[l])
        qd = _rope(qd, pos_s)
        kd = _rope(kd, pos_s)
        kd_past = cache_k_diff[l, page_table].reshape(DB, past, DIFF_HEADS, 2, DIFF_QK)
        vd_past = cache_v_diff[l, page_table].reshape(DB, past, DIFF_HEADS, HEAD_DIM)
        ks_past = cache_k_sb[l, page_table].reshape(DB, past, SB_HEADS, HEAD_DIM)
        vs_past = cache_v_sb[l, page_table].reshape(DB, past, SB_HEADS, HEAD_DIM)
        prefix = state_pool[l]
        a_out = _pool_mix(u, prefix, pos_s, w_pool[l], pool_scale[l])
        b_out = _diff_mix(qd, (kd_past, kd), (vd_past, vd), (past_vis, causal_new), lam_l)
        b_out = _rms(b_out, g_subln[l]) * (1.0 - lam_init)
        c_out = _stick_mix(qs, (ks_past, ks_), (vs_past, vs_), (past_vis, strict_new))
        xs = xs + _merge(a_out, b_out, c_out, w_out[l])
        sk_d.append(kd.reshape(DB, TQ, DIFF_HEADS, HEAD_DIM))
        sv_d.append(vd)
        sk_s.append(ks_)
        sv_s.append(vs_)
        s_pool.append(jnp.concatenate([prefix, u], axis=1)[:, TQ:])
        xs = xs + _cross(_rms(xs, g_x[l]), cache_mem_k[l], cache_mem_v[l], w_q_x[l], w_o_x[l])
        xs = xs + _mlp(_rms(xs, g_mlp[l]), w_up[l], w_down[l])

    y_prompt = _rms(xp, g_final)
    y_sample = _rms(xs, g_final)
    return (y_prompt, y_sample,
            jnp.stack(pk_d), jnp.stack(pv_d), jnp.stack(pk_s), jnp.stack(pv_s), jnp.stack(p_pool),
            jnp.stack(p_mk), jnp.stack(p_mv),
            jnp.stack(sk_d), jnp.stack(sv_d), jnp.stack(sk_s), jnp.stack(sv_s), jnp.stack(s_pool))
```

```python
import functools
import math

import jax
import jax.numpy as jnp
from jax import lax
from jax.experimental import pallas as pl
from jax.experimental.pallas import tpu as pltpu

F32 = jnp.float32
BF16 = jnp.bfloat16

LANE = 128
BF16_SUBLANES = 16
VMEM_LIMIT = 56 * 2**20
POOL_WINDOWS = (2, 4, 8, 16)
HALO = 16
ROPE_THETA = 10000.0
EPS = 1e-6
NEG_INF = -1e30
NT_DIMS = (((1,), (1,)), ((), ()))


def _params(*sem):
    return pltpu.CompilerParams(dimension_semantics=sem, vmem_limit_bytes=VMEM_LIMIT)


def _log_sigmoid(z):
    return jnp.minimum(z, 0.0) - jnp.log1p(jnp.exp(-jnp.abs(z)))


def _split_bf16(x):
    hi = x.astype(BF16)
    lo = (x - hi.astype(F32)).astype(BF16)
    return hi, lo


def _lanes(x):
    return jnp.broadcast_to(x, (x.shape[0], LANE))


def _rms_kernel(x_ref, g_ref, o_ref):
    x = x_ref[...]
    ms = jnp.mean(x * x, axis=-1, keepdims=True)
    o_ref[...] = (x * lax.rsqrt(ms + EPS) * g_ref[...]).astype(o_ref.dtype)


def _rmsnorm(x, g, out_dtype):
    m, d = x.shape
    tr = min(m, 256)
    return pl.pallas_call(
        _rms_kernel,
        grid=(m // tr,),
        in_specs=[pl.BlockSpec((tr, d), lambda i: (i, 0)), pl.BlockSpec((1, d), lambda i: (0, 0))],
        out_specs=pl.BlockSpec((tr, d), lambda i: (i, 0)),
        out_shape=jax.ShapeDtypeStruct((m, d), out_dtype),
        compiler_params=_params("parallel"),
        name="rmsnorm",
    )(x, g.reshape(1, d).astype(F32))


def _mm_kernel(*refs, nk, act, has_res, has_rope, out_kinds):
    a_ref, w_ref = refs[0], refs[1]
    pos = 2
    res_ref = cos_ref = sin_ref = None
    if has_res:
        res_ref = refs[pos]
        pos += 1
    if has_rope:
        cos_ref, sin_ref = refs[pos], refs[pos + 1]
        pos += 2
    out_refs = refs[pos:pos + len(out_kinds)]
    acc_ref = refs[pos + len(out_kinds)] if nk > 1 else None

    def finish(acc):
        if act == "relu2":
            r = jnp.maximum(acc, 0.0)
            acc = r * r
        if has_res:
            acc = acc + res_ref[...]
        if has_rope:
            cos, sin = cos_ref[...], sin_ref[...]
            quarter = LANE // 4
            lane = lax.broadcasted_iota(jnp.int32, cos.shape, 1)
            first_half = (lane % (2 * quarter)) < quarter
            parts = []
            for h in range(acc.shape[1] // LANE):
                xh = acc[:, h * LANE:(h + 1) * LANE]
                partner = jnp.where(first_half, pltpu.roll(xh, LANE - quarter, axis=1),
                                    pltpu.roll(xh, quarter, axis=1))
                parts.append(xh * cos + partner * sin)
            acc = jnp.concatenate(parts, axis=1)
        for o_ref, kind in zip(out_refs, out_kinds):
            if kind == "flat":
                o_ref[...] = acc.astype(o_ref.dtype)
            else:
                for h in range(acc.shape[1] // LANE):
                    o_ref[0, h] = acc[:, h * LANE:(h + 1) * LANE].astype(o_ref.dtype)

    part = jnp.dot(a_ref[...], w_ref[...], preferred_element_type=F32)
    if nk == 1:
        finish(part)
    else:
        k = pl.program_id(2)

        @pl.when(k == 0)
        def _():
            acc_ref[...] = part

        @pl.when(k > 0)
        def _():
            acc_ref[...] += part

        @pl.when(k == nk - 1)
        def _():
            finish(acc_ref[...])


def _matmul(a, w, *, col0=0, n=None, act=None, res=None, rope=None, outs=(("flat", F32),),
            seq=None, tm=1024, tn=512, tk=4096, name="matmul"):
    m, kdim = a.shape
    n = w.shape[1] if n is None else n
    tm, tn, tk = min(tm, m, seq or m), min(tn, n), min(tk, kdim)
    while n % tn or col0 % tn:
        tn -= LANE
    assert m % tm == 0 and n % tn == 0 and kdim % tk == 0 and col0 % tn == 0 and tn % LANE == 0
    nk = kdim // tk
    jb = col0 // tn
    in_specs = [pl.BlockSpec((tm, tk), lambda i, j, k: (i, k)),
                pl.BlockSpec((tk, tn), lambda i, j, k: (k, j + jb))]
    args = [a, w]
    if res is not None:
        in_specs.append(pl.BlockSpec((tm, tn), lambda i, j, k: (i, j)))
        args.append(res)
    if rope is not None:
        nt = rope[0].shape[0] // tm
        assert rope[0].shape[0] % tm == 0
        for t in rope:
            in_specs.append(pl.BlockSpec((tm, LANE), lambda i, j, k: (i % nt, 0)))
            args.append(t)
    out_specs, out_shapes = [], []
    for kind, dt in outs:
        if kind == "flat":
            out_specs.append(pl.BlockSpec((tm, tn), lambda i, j, k: (i, j)))
            out_shapes.append(jax.ShapeDtypeStruct((m, n), dt))
        else:
            assert seq % tm == 0
            tpb = seq // tm
            out_specs.append(pl.BlockSpec((1, tn // LANE, tm, LANE),
                                          lambda i, j, k: (i // tpb, j, i % tpb, 0)))
            out_shapes.append(jax.ShapeDtypeStruct((m // seq, n // LANE, seq, LANE), dt))
    kern = functools.partial(_mm_kernel, nk=nk, act=act, has_res=res is not None,
                             has_rope=rope is not None, out_kinds=tuple(k for k, _ in outs))
    result = pl.pallas_call(
        kern,
        grid=(m // tm, n // tn, nk),
        in_specs=in_specs,
        out_specs=out_specs,
        out_shape=out_shapes,
        scratch_shapes=[pltpu.VMEM((tm, tn), F32)] if nk > 1 else [],
        compiler_params=_params("parallel", "parallel", "arbitrary"),
        name=name,
    )(*args)
    return result[0] if len(outs) == 1 else result


def _pool_kernel(u_ref, halo_ref, pre_ref, w_ref, sc_ref, o_ref, ext_ref, *, tt, gc, pos0):
    i = pl.program_id(1)
    x = u_ref[0]
    ext_ref[HALO:HALO + tt, :] = x

    @pl.when(i == 0)
    def _():
        ext_ref[0:HALO, :] = pre_ref[0]

    @pl.when(i > 0)
    def _():
        ext_ref[0:HALO, :] = halo_ref[0]

    pos = pos0 + i * tt + lax.broadcasted_iota(jnp.int32, (tt, 1), 0)
    for g, win in enumerate(POOL_WINDOWS):
        cs = slice(g * gc, (g + 1) * gc)
        xg = x[:, cs]
        s = xg
        for back in range(1, win):
            s = s + ext_ref[HALO - back:HALO - back + tt, cs]
        cnt = jnp.minimum(pos + 1, win).astype(F32)
        r = (s / cnt - xg).astype(BF16)
        y = jnp.dot(r, w_ref[g], preferred_element_type=F32) * sc_ref[:, cs]
        o_ref[0, :, cs] = y.astype(o_ref.dtype)


def _pool_mix(u, prefix, w_pool, scale, pos0):
    b, t, pw = u.shape
    groups, gc, _ = w_pool.shape
    assert groups == len(POOL_WINDOWS)
    tt = min(t, 256)
    hb = tt // HALO
    return pl.pallas_call(
        functools.partial(_pool_kernel, tt=tt, gc=gc, pos0=pos0),
        grid=(b, t // tt),
        in_specs=[pl.BlockSpec((1, tt, pw), lambda bi, i: (bi, i, 0)),
                  pl.BlockSpec((1, HALO, pw), lambda bi, i: (bi, jnp.maximum(i * hb - 1, 0), 0)),
                  pl.BlockSpec((1, HALO, pw), lambda bi, i: (bi, 0, 0)),
                  pl.BlockSpec((groups, gc, gc), lambda bi, i: (0, 0, 0)),
                  pl.BlockSpec((1, pw), lambda bi, i: (0, 0))],
        out_specs=pl.BlockSpec((1, tt, pw), lambda bi, i: (bi, i, 0)),
        out_shape=jax.ShapeDtypeStruct((b, t, pw), BF16),
        scratch_shapes=[pltpu.VMEM((HALO + tt, pw), F32)],
        compiler_params=_params("parallel", "arbitrary"),
        name="pool_mix",
    )(u, u, prefix, w_pool, scale.reshape(1, pw).astype(F32))


def _lam_value(lam_ref, lam_init):
    lam = lam_ref[...]
    s1 = jnp.sum(lam[0:1] * lam[1:2], axis=1, keepdims=True)
    s2 = jnp.sum(lam[2:3] * lam[3:4], axis=1, keepdims=True)
    return jnp.exp(s1) - jnp.exp(s2) + lam_init


def _diff_finish(acc0, l0, acc1, l1, lam_l, g, lam_init):
    o = acc0 / l0 - lam_l * (acc1 / l1)
    ms = jnp.mean(o * o, axis=-1, keepdims=True)
    return o * lax.rsqrt(ms + EPS) * g * (1.0 - lam_init)


def _split_maps(q):
    lane = lax.broadcasted_iota(jnp.int32, q.shape, 1)
    zero = jnp.zeros_like(q)
    return jnp.where(lane < LANE // 2, q, zero), jnp.where(lane >= LANE // 2, q, zero)


def _diff_attn_kernel(q_ref, k_ref, v_ref, lam_ref, g_ref, o_ref, m_ref, l_ref, acc_ref, *,
                      tq, scale, lam_init):
    i = pl.program_id(2)
    qc = _split_maps(q_ref[0, 0])
    m_ref[...] = jnp.full(m_ref.shape, NEG_INF, F32)
    l_ref[...] = jnp.zeros(l_ref.shape, F32)
    acc_ref[...] = jnp.zeros(acc_ref.shape, F32)

    def block(j, masked):
        start = pl.multiple_of(j * tq, tq)
        kj = k_ref[0, 0, pl.ds(start, tq), :]
        vj = v_ref[0, 0, pl.ds(start, tq), :]
        for c in range(2):
            s = lax.dot_general(qc[c], kj, NT_DIMS, preferred_element_type=F32) * scale
            if masked:
                row = lax.broadcasted_iota(jnp.int32, s.shape, 0)
                col = lax.broadcasted_iota(jnp.int32, s.shape, 1)
                s = jnp.where(col <= row, s, NEG_INF)
            m_prev = m_ref[c][:, :1]
            m_new = jnp.maximum(m_prev, jnp.max(s, axis=-1, keepdims=True))
            alpha = jnp.exp(m_prev - m_new)
            p = jnp.exp(s - m_new)
            l_ref[c] = _lanes(alpha * l_ref[c][:, :1] + jnp.sum(p, axis=-1, keepdims=True))
            m_ref[c] = _lanes(m_new)
            acc_ref[c] = alpha * acc_ref[c] + jnp.dot(p.astype(BF16), vj, preferred_element_type=F32)

    def body(j, carry):
        block(j, False)
        return carry

    lax.fori_loop(0, i, body, 0)
    block(i, True)
    out = _diff_finish(acc_ref[0], l_ref[0][:, :1], acc_ref[1], l_ref[1][:, :1],
                       _lam_value(lam_ref, lam_init), g_ref[...], lam_init)
    o_ref[0] = out.astype(o_ref.dtype)


def _diff_attn(q, k, v, lam, g, lam_init, tq):
    b, h, t, _ = q.shape
    kern = functools.partial(_diff_attn_kernel, tq=tq, scale=(LANE // 2) ** -0.5, lam_init=lam_init)
    return pl.pallas_call(
        kern,
        grid=(b, h, t // tq),
        in_specs=[pl.BlockSpec((1, 1, tq, LANE), lambda bi, hi, i: (bi, hi, i, 0)),
                  pl.BlockSpec((1, 1, t, LANE), lambda bi, hi, i: (bi, hi, 0, 0)),
                  pl.BlockSpec((1, 1, t, LANE), lambda bi, hi, i: (bi, hi, 0, 0)),
                  pl.BlockSpec(lam.shape, lambda bi, hi, i: (0, 0)),
                  pl.BlockSpec((1, LANE), lambda bi, hi, i: (0, 0))],
        out_specs=pl.BlockSpec((1, tq, LANE), lambda bi, hi, i: (bi, i, hi)),
        out_shape=jax.ShapeDtypeStruct((b, t, h * LANE), BF16),
        scratch_shapes=[pltpu.VMEM((2, tq, LANE), F32), pltpu.VMEM((2, tq, LANE), F32),
                        pltpu.VMEM((2, tq, LANE), F32)],
        compiler_params=_params("parallel", "parallel", "arbitrary"),
        name="diff_attn",
    )(q, k, v, lam.astype(F32), g.reshape(1, LANE).astype(F32))


def _sb_block(z, u, carry, mask):
    lb = _log_sigmoid(z)
    lk = lb - z
    if mask is not None:
        lk = jnp.where(mask, lk, 0.0)
    hi, lo = _split_bf16(lk)
    after = (jnp.dot(hi, u, preferred_element_type=F32) + jnp.dot(lo, u, preferred_element_type=F32)) + carry
    a = jnp.exp(lb + after)
    if mask is not None:
        a = jnp.where(mask, a, 0.0)
    return a, jnp.sum(lk, axis=-1, keepdims=True)


def _sb_attn_kernel(q_ref, k_ref, v_ref, u_ref, o_ref, c_ref, acc_ref, *, tq, scale):
    i = pl.program_id(2)
    q = q_ref[0, 0]
    c_ref[...] = jnp.zeros(c_ref.shape, F32)
    acc_ref[...] = jnp.zeros(acc_ref.shape, F32)

    def block(j, masked):
        start = pl.multiple_of(j * tq, tq)
        kj = k_ref[0, 0, pl.ds(start, tq), :]
        vj = v_ref[0, 0, pl.ds(start, tq), :]
        z = lax.dot_general(q, kj, NT_DIMS, preferred_element_type=F32) * scale
        mask = None
        if masked:
            row = lax.broadcasted_iota(jnp.int32, z.shape, 0)
            col = lax.broadcasted_iota(jnp.int32, z.shape, 1)
            mask = col < row
        a, lk_sum = _sb_block(z, u_ref[...], c_ref[:, :1], mask)
        acc_ref[...] += jnp.dot(a.astype(BF16), vj, preferred_element_type=F32)
        c_ref[...] = c_ref[...] + lk_sum

    block(i, True)

    def body(t, carry):
        block(i - 1 - t, False)
        return carry

    lax.fori_loop(0, i, body, 0)
    o_ref[0] = acc_ref[...].astype(o_ref.dtype)


def _sb_attn(q, k, v, u, tq):
    b, h, t, _ = q.shape
    return pl.pallas_call(
        functools.partial(_sb_attn_kernel, tq=tq, scale=LANE ** -0.5),
        grid=(b, h, t // tq),
        in_specs=[pl.BlockSpec((1, 1, tq, LANE), lambda bi, hi, i: (bi, hi, i, 0)),
                  pl.BlockSpec((1, 1, t, LANE), lambda bi, hi, i: (bi, hi, 0, 0)),
                  pl.BlockSpec((1, 1, t, LANE), lambda bi, hi, i: (bi, hi, 0, 0)),
                  pl.BlockSpec((tq, tq), lambda bi, hi, i: (0, 0))],
        out_specs=pl.BlockSpec((1, tq, LANE), lambda bi, hi, i: (bi, i, hi)),
        out_shape=jax.ShapeDtypeStruct((b, t, h * LANE), BF16),
        scratch_shapes=[pltpu.VMEM((tq, LANE), F32), pltpu.VMEM((tq, LANE), F32)],
        compiler_params=_params("parallel", "parallel", "arbitrary"),
        name="sb_attn",
    )(q, k, v, u)


def _cross_kernel(q_ref, k_ref, v_ref, o_ref, *, nh, scale):
    for h in range(nh):
        sl = slice(h * LANE, (h + 1) * LANE)
        s = lax.dot_general(q_ref[0, :, sl], k_ref[0, :, sl], NT_DIMS, preferred_element_type=F32) * scale
        p = jnp.exp(s - jnp.max(s, axis=-1, keepdims=True))
        l = jnp.sum(p, axis=-1, keepdims=True)
        o = jnp.dot(p.astype(BF16), v_ref[0, :, sl], preferred_element_type=F32) / l
        o_ref[0, :, sl] = o.astype(o_ref.dtype)


def _cross_attn(q, k, v):
    b, t, xw = q.shape
    mt = k.shape[1]
    tq = min(t, 512)
    return pl.pallas_call(
        functools.partial(_cross_kernel, nh=xw // LANE, scale=LANE ** -0.5),
        grid=(b, t // tq),
        in_specs=[pl.BlockSpec((1, tq, xw), lambda bi, i: (bi, i, 0)),
                  pl.BlockSpec((1, mt, xw), lambda bi, i: (bi, 0, 0)),
                  pl.BlockSpec((1, mt, xw), lambda bi, i: (bi, 0, 0))],
        out_specs=pl.BlockSpec((1, tq, xw), lambda bi, i: (bi, i, 0)),
        out_shape=jax.ShapeDtypeStruct((b, t, xw), BF16),
        compiler_params=_params("parallel", "parallel"),
        name="cross_attn",
    )(q, k, v)


def _page_specs(layer, n_pages, ppc, heads, page, reverse):
    specs = []
    for j in range(ppc):
        def index(b, c, pt, j=j):
            first = (n_pages - (c + 1) * ppc) if reverse else c * ppc
            return (layer, pt[b * n_pages + first + j], 0, 0, 0)
        specs.append(pl.BlockSpec((1, 1, heads, page, LANE), index))
    return specs


def _chunk_of(refs, h):
    return jnp.concatenate([r[0, 0, h] for r in refs], axis=0).astype(BF16)


def _diff_decode_kernel(pt_ref, q_ref, kn_ref, vn_ref, lam_ref, g_ref, *rest, ppc, nh, rows, scale, lam_init):
    k_refs, v_refs = rest[:ppc], rest[ppc:2 * ppc]
    o_ref, m_ref, l_ref, acc_ref = rest[2 * ppc:]
    c = pl.program_id(1)

    def queries(h):
        q0, q1 = _split_maps(q_ref[0, :, h * LANE:(h + 1) * LANE])
        return jnp.concatenate([q0, q1], axis=0).astype(BF16)

    def update(h, s, v):
        m_prev = m_ref[h][:, :1]
        m_new = jnp.maximum(m_prev, jnp.max(s, axis=-1, keepdims=True))
        alpha = jnp.exp(m_prev - m_new)
        p = jnp.exp(s - m_new)
        l_ref[h] = _lanes(alpha * l_ref[h][:, :1] + jnp.sum(p, axis=-1, keepdims=True))
        m_ref[h] = _lanes(m_new)
        acc_ref[h] = alpha * acc_ref[h] + jnp.dot(p.astype(BF16), v, preferred_element_type=F32)

    @pl.when(c == 0)
    def _():
        m_ref[...] = jnp.full(m_ref.shape, NEG_INF, F32)
        l_ref[...] = jnp.zeros(l_ref.shape, F32)
        acc_ref[...] = jnp.zeros(acc_ref.shape, F32)
        shape = (2 * rows, kn_ref.shape[2])
        t = lax.broadcasted_iota(jnp.int32, shape, 0) % rows
        visible = lax.broadcasted_iota(jnp.int32, shape, 1) <= t
        for h in range(nh):
            s = lax.dot_general(queries(h), kn_ref[0, h], NT_DIMS, preferred_element_type=F32) * scale
            update(h, jnp.where(visible, s, NEG_INF), vn_ref[0, h])

    for h in range(nh):
        s = lax.dot_general(queries(h), _chunk_of(k_refs, h), NT_DIMS, preferred_element_type=F32) * scale
        update(h, s, _chunk_of(v_refs, h))

    @pl.when(c == pl.num_programs(1) - 1)
    def _():
        lam_l = _lam_value(lam_ref, lam_init)
        for h in range(nh):
            acc, l = acc_ref[h], l_ref[h][:, :1]
            o_ref[0, :, h * LANE:(h + 1) * LANE] = _diff_finish(
                acc[:rows], l[:rows], acc[rows:], l[rows:], lam_l, g_ref[...], lam_init)


def _sb_decode_kernel(pt_ref, q_ref, kn_ref, vn_ref, u_ref, *rest, ppc, nh, rows, scale):
    k_refs, v_refs = rest[:ppc], rest[ppc:2 * ppc]
    o_ref, c_ref, acc_ref = rest[2 * ppc:]
    c = pl.program_id(1)

    def chunk(keys_of, values_of, nkeys, mask):
        z = jnp.concatenate(
            [lax.dot_general(q_ref[0, :, h * LANE:(h + 1) * LANE].astype(BF16), keys_of(h), NT_DIMS,
                             preferred_element_type=F32) for h in range(nh)], axis=0) * scale
        a, lk_sum = _sb_block(z, u_ref[:nkeys, :nkeys], c_ref[:, :1], mask)
        a = a.astype(BF16)
        for h in range(nh):
            acc_ref[h] += jnp.dot(a[h * rows:(h + 1) * rows], values_of(h), preferred_element_type=F32)
        c_ref[...] = c_ref[...] + lk_sum

    @pl.when(c == 0)
    def _():
        c_ref[...] = jnp.zeros(c_ref.shape, F32)
        acc_ref[...] = jnp.zeros(acc_ref.shape, F32)
        nkeys = kn_ref.shape[2]
        shape = (nh * rows, nkeys)
        t = lax.broadcasted_iota(jnp.int32, shape, 0) % rows
        mask = lax.broadcasted_iota(jnp.int32, shape, 1) < t
        chunk(lambda h: kn_ref[0, h], lambda h: vn_ref[0, h], nkeys, mask)

    chunk(lambda h: _chunk_of(k_refs, h), lambda h: _chunk_of(v_refs, h), ppc * k_refs[0].shape[3], None)

    @pl.when(c == pl.num_programs(1) - 1)
    def _():
        for h in range(nh):
            o_ref[0, :, h * LANE:(h + 1) * LANE] = acc_ref[h]


def _decode_attn(kind, layer, q, k_new, v_new, k_cache, v_cache, pt, n_pages, extra, lam_init=None):
    db, rows, width = q.shape
    heads, page = k_cache.shape[2], k_cache.shape[3]
    ppc = 4 if n_pages % 4 == 0 else 1
    reverse = kind == "sb"
    pages = _page_specs(layer, n_pages, ppc, heads, page, reverse)
    whole = lambda shape: pl.BlockSpec(shape, lambda b, c, pt: (0,) * len(shape))
    in_specs = [pl.BlockSpec((1, rows, width), lambda b, c, pt: (b, 0, 0)),
                pl.BlockSpec((1,) + k_new.shape[1:], lambda b, c, pt: (b, 0, 0, 0)),
                pl.BlockSpec((1,) + v_new.shape[1:], lambda b, c, pt: (b, 0, 0, 0))]
    in_specs += [whole(e.shape) for e in extra] + pages + pages
    if kind == "diff":
        kern = functools.partial(_diff_decode_kernel, ppc=ppc, nh=heads, rows=rows,
                                 scale=(LANE // 2) ** -0.5, lam_init=lam_init)
        scratch = [pltpu.VMEM((heads, 2 * rows, LANE), F32)] * 3
    else:
        kern = functools.partial(_sb_decode_kernel, ppc=ppc, nh=heads, rows=rows, scale=LANE ** -0.5)
        scratch = [pltpu.VMEM((heads * rows, LANE), F32), pltpu.VMEM((heads, rows, LANE), F32)]
    return pl.pallas_call(
        kern,
        grid_spec=pltpu.PrefetchScalarGridSpec(
            num_scalar_prefetch=1,
            grid=(db, n_pages // ppc),
            in_specs=in_specs,
            out_specs=pl.BlockSpec((1, rows, width), lambda b, c, pt: (b, 0, 0)),
            scratch_shapes=scratch),
        out_shape=jax.ShapeDtypeStruct((db, rows, width), F32),
        compiler_params=_params("parallel", "arbitrary"),
        name=kind + "_decode",
    )(pt, q, k_new, v_new, *extra, *([k_cache] * ppc), *([v_cache] * ppc))


def _rope_tables(pos):
    half = LANE // 4
    inv = ROPE_THETA ** (-jnp.arange(half, dtype=F32) / half)
    ang = pos.astype(F32)[:, None] * inv[None, :]
    cos, sin = jnp.cos(ang), jnp.sin(ang)
    return jnp.tile(jnp.concatenate([cos, cos], axis=1), (1, 2)), jnp.tile(jnp.concatenate([-sin, sin], axis=1), (1, 2))


def _strict_lower(n):
    i = lax.broadcasted_iota(jnp.int32, (n, n), 0)
    j = lax.broadcasted_iota(jnp.int32, (n, n), 1)
    return (i > j).astype(BF16)


def _new_keys(x, db, rows, heads, page):
    x = x.reshape(db, rows, heads, LANE).transpose(0, 2, 1, 3).astype(BF16)
    return jnp.pad(x, ((0, 0), (0, 0), (0, page - rows), (0, 0)))


def kernel(x_prompt, x_sample, mem_prompt, cache_k_diff, cache_v_diff, cache_k_sb, cache_v_sb, state_pool, cache_mem_k, cache_mem_v, page_table, g_mix, w_in, w_pool, pool_scale, lam, g_subln, w_out, g_x, g_mem, w_q_x, w_kv_x, w_o_x, g_mlp, w_up, w_down, g_final):
    b, t, d = x_prompt.shape
    db, tq, _ = x_sample.shape
    depth = w_in.shape[0]
    n_state, pw = state_pool.shape[2], state_pool.shape[3]
    hd, hs = cache_k_diff.shape[3], cache_k_sb.shape[3]
    dw, sw = hd * LANE, hs * LANE
    page, n_pages = cache_k_diff.shape[2], page_table.shape[1]
    past = n_pages * page
    mt, xw = mem_prompt.shape[1], cache_mem_k.shape[3] * cache_mem_k.shape[4]
    assert cache_k_diff.shape[4] == LANE and cache_k_sb.shape[4] == LANE and cache_mem_k.shape[4] == LANE
    assert n_state < HALO and tq <= BF16_SUBLANES
    rows = BF16_SUBLANES
    m, ms = b * t, db * rows
    col = {"u": 0, "qd": pw, "kd": pw + dw, "vd": pw + 2 * dw,
           "qs": pw + 3 * dw, "ks": pw + 3 * dw + sw, "vs": pw + 3 * dw + 2 * sw}
    tq_attn = min(t, 256)

    rope_p = _rope_tables(jnp.arange(t, dtype=jnp.int32))
    rope_s = tuple(jnp.tile(tab, (db, 1)) for tab in _rope_tables(past + jnp.arange(rows, dtype=jnp.int32)))
    u_attn = _strict_lower(tq_attn)
    u_dec = _strict_lower(4 * page if n_pages % 4 == 0 else page)
    pt = page_table.reshape(-1)
    ck_d, cv_d, ck_s, cv_s = (c.transpose(0, 1, 3, 2, 4) for c in (cache_k_diff, cache_v_diff, cache_k_sb, cache_v_sb))

    xp = x_prompt.reshape(m, d)
    xs = jnp.pad(x_sample, ((0, 0), (0, rows - tq), (0, 0))).reshape(ms, d)
    mem = mem_prompt.reshape(b * mt, d)
    zero_prefix = jnp.zeros((b, HALO, pw), F32)
    outs = {k: [] for k in ("pk_d", "pv_d", "pk_s", "pv_s", "p_pool", "p_mk", "p_mv",
                            "sk_d", "sv_d", "sk_s", "sv_s", "s_pool")}
    head_f32_bf16 = (("heads", F32), ("heads", BF16))

    for l in range(depth):
        lam_init = 0.8 - 0.6 * math.exp(-0.3 * l)
        w_in_l, w_out_l, w_pool_l = w_in[l].astype(BF16), w_out[l].astype(BF16), w_pool[l].astype(BF16)
        w_q_l, w_kv_l, w_o_l = w_q_x[l].astype(BF16), w_kv_x[l].astype(BF16), w_o_x[l].astype(BF16)
        w_up_l, w_down_l = w_up[l].astype(BF16), w_down[l].astype(BF16)

        xn = _rmsnorm(xp, g_mix[l], BF16)
        proj = functools.partial(_matmul, xn, w_in_l, seq=t)
        u = proj(col0=col["u"], n=pw, name="proj_u")
        qd = proj(col0=col["qd"], n=dw, rope=rope_p, outs=(("heads", BF16),), name="proj_qd")
        kd, kd_b = proj(col0=col["kd"], n=dw, rope=rope_p, outs=head_f32_bf16, name="proj_kd")
        vd, vd_b = proj(col0=col["vd"], n=dw, outs=head_f32_bf16, name="proj_vd")
        qs = proj(col0=col["qs"], n=sw, outs=(("heads", BF16),), name="proj_qs")
        ks, ks_b = proj(col0=col["ks"], n=sw, outs=head_f32_bf16, name="proj_ks")
        vs, vs_b = proj(col0=col["vs"], n=sw, outs=head_f32_bf16, name="proj_vs")
        u3 = u.reshape(b, t, pw)
        a_out = _pool_mix(u3, zero_prefix, w_pool_l, pool_scale[l], 0)
        b_out = _diff_attn(qd, kd_b, vd_b, lam[l], g_subln[l], lam_init, tq_attn)
        c_out = _sb_attn(qs, ks_b, vs_b, u_attn, tq_attn)
        cat = jnp.concatenate([a_out, b_out, c_out], axis=-1).reshape(m, d)
        xp = _matmul(cat, w_out_l, res=xp, name="merge")
        outs["pk_d"].append(kd), outs["pv_d"].append(vd), outs["pk_s"].append(ks), outs["pv_s"].append(vs)
        outs["p_pool"].append(u3[:, t - n_state:])
        kv, kv_b = _matmul(_rmsnorm(mem, g_mem[l], BF16), w_kv_l, outs=(("flat", F32), ("flat", BF16)), name="mem_kv")
        outs["p_mk"].append(kv[:, :xw].reshape(b, mt, xw // LANE, LANE))
        outs["p_mv"].append(kv[:, xw:].reshape(b, mt, xw // LANE, LANE))
        qx = _matmul(_rmsnorm(xp, g_x[l], BF16), w_q_l, outs=(("flat", BF16),), name="cross_q")
        ox = _cross_attn(qx.reshape(b, t, xw), kv_b[:, :xw].reshape(b, mt, xw), kv_b[:, xw:].reshape(b, mt, xw))
        xp = _matmul(ox.reshape(m, xw), w_o_l, res=xp, name="cross_o")
        hmid = _matmul(_rmsnorm(xp, g_mlp[l], BF16), w_up_l, act="relu2", outs=(("flat", BF16),), name="mlp_up")
        xp = _matmul(hmid, w_down_l, res=xp, tn=1024, tk=2048, name="mlp_down")

        xn = _rmsnorm(xs, g_mix[l], BF16)
        proj = functools.partial(_matmul, xn, w_in_l)
        u = proj(col0=col["u"], n=pw, name="s_proj_u")
        qd = proj(col0=col["qd"], n=dw, rope=rope_s, name="s_proj_qd")
        kd = proj(col0=col["kd"], n=dw, rope=rope_s, name="s_proj_kd")
        vd = proj(col0=col["vd"], n=dw, name="s_proj_vd")
        qs = proj(col0=col["qs"], n=sw, name="s_proj_qs")
        ks = proj(col0=col["ks"], n=sw, name="s_proj_ks")
        vs = proj(col0=col["vs"], n=sw, name="s_proj_vs")
        u3 = u.reshape(db, rows, pw)
        prefix = jnp.pad(state_pool[l], ((0, 0), (HALO - n_state, 0), (0, 0)))
        a_out = _pool_mix(u3, prefix, w_pool_l, pool_scale[l], past)
        b_out = _decode_attn("diff", l, qd.reshape(db, rows, dw), _new_keys(kd, db, rows, hd, page),
                             _new_keys(vd, db, rows, hd, page), ck_d, cv_d, pt, n_pages,
                             (lam[l].astype(F32), g_subln[l].reshape(1, LANE).astype(F32)), lam_init)
        c_out = _decode_attn("sb", l, qs.reshape(db, rows, sw), _new_keys(ks, db, rows, hs, page),
                             _new_keys(vs, db, rows, hs, page), ck_s, cv_s, pt, n_pages, (u_dec,))
        cat = jnp.concatenate([a_out, b_out.astype(BF16), c_out.astype(BF16)], axis=-1).reshape(ms, d)
        xs = _matmul(cat, w_out_l, res=xs, name="s_merge")
        for key, val, heads in (("sk_d", kd, hd), ("sv_d", vd, hd), ("sk_s", ks, hs), ("sv_s", vs, hs)):
            outs[key].append(val.reshape(db, rows, heads, LANE)[:, :tq])
        outs["s_pool"].append(jnp.concatenate([state_pool[l], u3[:, :tq]], axis=1)[:, tq:])
        qx = _matmul(_rmsnorm(xs, g_x[l], BF16), w_q_l, outs=(("flat", BF16),), name="s_cross_q")
        ox = _cross_attn(qx.reshape(db, rows, xw), cache_mem_k[l].reshape(db, mt, xw).astype(BF16),
                         cache_mem_v[l].reshape(db, mt, xw).astype(BF16))
        xs = _matmul(ox.reshape(ms, xw), w_o_l, res=xs, name="s_cross_o")
        hmid = _matmul(_rmsnorm(xs, g_mlp[l], BF16), w_up_l, act="relu2", outs=(("flat", BF16),), name="s_mlp_up")
        xs = _matmul(hmid, w_down_l, res=xs, tn=1024, tk=2048, name="s_mlp_down")

    y_prompt = _rmsnorm(xp, g_final, F32).reshape(b, t, d)
    y_sample = _rmsnorm(xs, g_final, F32).reshape(db, rows, d)[:, :tq]
    heads_out = lambda key: jnp.stack(outs[key]).transpose(0, 1, 3, 2, 4)
    stack = lambda key: jnp.stack(outs[key])
    return (y_prompt, y_sample,
            heads_out("pk_d"), heads_out("pv_d"), heads_out("pk_s"), heads_out("pv_s"), stack("p_pool"),
            stack("p_mk"), stack("p_mv"),
            stack("sk_d"), stack("sv_d"), stack("sk_s"), stack("sv_s"), stack("s_pool"))
```
